```python
import math
import jax, jax.numpy as jnp
from jax import lax
import numpy as np

D_MODEL = 2048
BATCH = 2
SEQ = 16384
DEPTH = 1

MEM_LEN = 256
EPS = 1e-6
DA_HEADS = 8
DA_DK = 128
DA_DV = 256
Q_BLOCK = 128
ML_HEADS = 4
ML_D = 256
ML_CHUNK = 64
CONV_W = 4
XA_HEADS = 4
XA_D = 256
N_BRANCH = 3
N_GROUPS = 4
EXPERTS_PER_GROUP = 8
N_EXPERTS = N_GROUPS * EXPERTS_PER_GROUP
TOP_K = 2
D_EXPERT = 704
MOE_BLOCK = 128

SPLIT_SIZES = (
    DA_HEADS * 2 * DA_DK,
    DA_HEADS * 2 * DA_DK,
    DA_HEADS * DA_DV,
    2 * ML_HEADS * ML_D,
    ML_HEADS * ML_D,
    ML_HEADS * ML_D,
    2 * ML_HEADS,
    XA_HEADS * XA_D,
    N_BRANCH * D_MODEL,
)
D_IN = sum(SPLIT_SIZES)

kernel_name = "hybrid_diffattn_mlstm_memxattn_hmoe"


def rms_norm(x, g):
    x32 = x.astype(jnp.float32)
    y = x32 * lax.rsqrt(jnp.mean(x32 * x32, axis=-1, keepdims=True) + EPS)
    return (y * g.astype(jnp.float32)).astype(x.dtype)


def causal_conv(x, w, b):
    S = x.shape[1]
    xp = jnp.pad(x, ((0, 0), (CONV_W - 1, 0), (0, 0)))
    return sum(xp[:, j:j + S] * w[j] for j in range(CONV_W)) + b


def diff_attention(q, k, v, lam, slopes):
    B, S, H = q.shape[:3]
    q = jnp.transpose(q, (0, 2, 3, 1, 4)).astype(jnp.float32)
    k = jnp.transpose(k, (0, 2, 3, 1, 4)).astype(jnp.float32)
    v = jnp.transpose(v, (0, 2, 1, 3)).astype(jnp.float32)
    nb = S // Q_BLOCK
    qb = jnp.moveaxis(q.reshape(B, H, 2, nb, Q_BLOCK, DA_DK), 3, 0)
    key_pos = jnp.arange(S)
    scale = DA_DK ** -0.5

    def block(args):
        qi, i = args
        s = jnp.einsum('bhmqd,bhmkd->bhmqk', qi, k) * scale
        q_pos = i * Q_BLOCK + jnp.arange(Q_BLOCK)
        dist = q_pos[:, None] - key_pos[None, :]
        bias = -slopes[None, :, None, None, None] * dist.astype(jnp.float32)
        s = jnp.where(dist >= 0, s + bias, -jnp.inf)
        p = jax.nn.softmax(s, axis=-1)
        a = p[:, :, 0] - lam * p[:, :, 1]
        return jnp.einsum('bhqk,bhkd->bhqd', a, v)

    o = lax.map(block, (qb, jnp.arange(nb)))
    return jnp.transpose(o, (1, 0, 3, 2, 4)).reshape(B, S, H, DA_DV)


def mlstm_chunkwise(q, k, v, ig, lf):
    B, S, H, d = q.shape
    nc = S // ML_CHUNK

    def chunks(t):
        t = t.astype(jnp.float32).reshape((B, nc, ML_CHUNK, H) + t.shape[3:])
        return jnp.moveaxis(jnp.moveaxis(t, 1, 0), 3, 2)

    causal = jnp.tril(jnp.ones((ML_CHUNK, ML_CHUNK), dtype=bool))

    def step(carry, xs):
        C, n, m = carry
        qc, kc, vc, ic, fc = xs
        b = jnp.cumsum(fc, axis=-1)
        log_d = jnp.where(causal, b[..., :, None] - b[..., None, :] + ic[..., None, :], -jnp.inf)
        m_inter = b + m[..., None]
        m_t = jnp.maximum(m_inter, jnp.max(log_d, axis=-1))
        dmat = jnp.exp(log_d - m_t[..., None])
        inter = jnp.exp(m_inter - m_t)
        w = jnp.einsum('bhtd,bhsd->bhts', qc, kc) * dmat
        num = jnp.einsum('bhts,bhsd->bhtd', w, vc) + inter[..., None] * jnp.einsum('bhtd,bhde->bhte', qc, C)
        nq = jnp.sum(w, axis=-1) + inter * jnp.einsum('bhtd,bhd->bht', qc, n)
        h = num / jnp.maximum(jnp.abs(nq), jnp.exp(-m_t))[..., None]
        b_last = b[..., -1]
        a = b_last[..., None] - b + ic
        m_new = jnp.maximum(b_last + m, jnp.max(a, axis=-1))
        decay = jnp.exp(b_last + m - m_new)
        wa = jnp.exp(a - m_new[..., None])
        C_new = decay[..., None, None] * C + jnp.einsum('bhs,bhsd,bhse->bhde', wa, kc, vc)
        n_new = decay[..., None] * n + jnp.einsum('bhs,bhsd->bhd', wa, kc)
        return (C_new, n_new, m_new), h

    init = (jnp.zeros((B, H, d, d), jnp.float32), jnp.zeros((B, H, d), jnp.float32),
            jnp.zeros((B, H), jnp.float32))
    _, h = lax.scan(step, init, (chunks(q), chunks(k), chunks(v), chunks(ig), chunks(lf)))
    return jnp.transpose(h, (1, 0, 3, 2, 4)).reshape(B, S, H, d)


def memory_cross_attention(q, mk, mv):
    s = jnp.einsum('bshd,bmhd->bhsm', q.astype(jnp.float32), mk.astype(jnp.float32)) * (XA_D ** -0.5)
    p = jax.nn.softmax(s, axis=-1)
    return jnp.einsum('bhsm,bmhd->bshd', p, mv.astype(jnp.float32))


def hierarchical_moe(h, w_router_group, b_router_group, w_router_expert, b_router_expert,
                     w_expert_gate, w_expert_up, w_expert_down):
    B, S, D = h.shape
    T = B * S
    hf = h.reshape(T, D)
    g_logits = (hf @ w_router_group).astype(jnp.float32) + b_router_group.astype(jnp.float32)
    g_prob = jax.nn.softmax(g_logits, axis=-1)
    grp = jnp.argmax(g_logits, axis=-1)
    p_grp = jnp.take_along_axis(g_prob, grp[:, None], axis=-1)
    e_logits = ((hf @ w_router_expert).astype(jnp.float32)
                + b_router_expert.astype(jnp.float32)).reshape(T, N_GROUPS, EXPERTS_PER_GROUP)
    e_in = jnp.take_along_axis(e_logits, grp[:, None, None], axis=1)[:, 0]
    top_v, top_i = lax.top_k(e_in, TOP_K)
    wts = jax.nn.softmax(top_v, axis=-1) * p_grp
    eid = (grp[:, None] * EXPERTS_PER_GROUP + top_i).reshape(-1).astype(jnp.int32)
    tok = jnp.repeat(jnp.arange(T, dtype=jnp.int32), TOP_K)
    wflat = wts.reshape(-1)
    n_assign = T * TOP_K
    order = jnp.argsort(eid)
    eid_s, tok_s, w_s = eid[order], tok[order], wflat[order]
    counts = jax.ops.segment_sum(jnp.ones_like(eid), eid, num_segments=N_EXPERTS)
    start = jnp.cumsum(counts) - counts
    padded = (counts + MOE_BLOCK - 1) // MOE_BLOCK * MOE_BLOCK
    pad_end = jnp.cumsum(padded)
    pad_start = pad_end - padded
    dest = pad_start[eid_s] + (jnp.arange(n_assign, dtype=jnp.int32) - start[eid_s])
    nb = (n_assign + MOE_BLOCK - 1) // MOE_BLOCK + N_EXPERTS
    P = nb * MOE_BLOCK
    buf_tok = jnp.zeros((P,), jnp.int32).at[dest].set(tok_s)
    buf_w = jnp.zeros((P,), jnp.float32).at[dest].set(w_s)
    blk_e = jnp.minimum(jnp.searchsorted(pad_end, jnp.arange(nb, dtype=jnp.int32) * MOE_BLOCK, side='right'),
                        N_EXPERTS - 1)

    def expert_block(args):
        tk, wk, e = args
        xb = hf[tk]
        a = xb @ w_expert_gate[e]
        u = xb @ w_expert_up[e]
        return ((jax.nn.silu(a) * u) @ w_expert_down[e]) * wk[:, None].astype(xb.dtype)

    out = lax.map(expert_block, (buf_tok.reshape(nb, MOE_BLOCK), buf_w.reshape(nb, MOE_BLOCK), blk_e))
    y = jnp.zeros_like(hf).at[buf_tok].add(out.reshape(P, D))
    return y.reshape(B, S, D)


def hybrid_layer(x, mem, layer_idx, g_mix, w_in, conv_w, conv_b, b_igate, b_fgate,
                 lam_q1, lam_k1, lam_q2, lam_k2, g_diff_head, g_mlstm_head, g_mem, w_mem_kv,
                 w_branch_diff, w_branch_mlstm, w_branch_cross, b_gate, w_out, g_ffn,
                 w_router_group, b_router_group, w_router_expert, b_router_expert,
                 w_expert_gate, w_expert_up, w_expert_down):
    B, S, _ = x.shape
    h = rms_norm(x, g_mix)
    proj = h @ w_in
    points = np.cumsum(SPLIT_SIZES)[:-1].tolist()
    da_q, da_k, da_v, ml_qk, ml_v, ml_o, ml_g, xa_q, gate_pre = jnp.split(proj, points, axis=-1)

    lam_init = 0.8 - 0.6 * math.exp(-0.3 * layer_idx)
    lam = (jnp.exp(jnp.dot(lam_q1.astype(jnp.float32), lam_k1.astype(jnp.float32)))
           - jnp.exp(jnp.dot(lam_q2.astype(jnp.float32), lam_k2.astype(jnp.float32))) + lam_init)
    slopes = 2.0 ** (-8.0 * jnp.arange(1, DA_HEADS + 1, dtype=jnp.float32) / DA_HEADS)
    da = diff_attention(da_q.reshape(B, S, DA_HEADS, 2, DA_DK), da_k.reshape(B, S, DA_HEADS, 2, DA_DK),
                        da_v.reshape(B, S, DA_HEADS, DA_DV), lam, slopes)
    da = (rms_norm(da, g_diff_head) * (1.0 - lam_init)).reshape(B, S, DA_HEADS * DA_DV).astype(x.dtype)

    qk = jax.nn.silu(causal_conv(ml_qk, conv_w, conv_b))
    ml_q, ml_k = jnp.split(qk, 2, axis=-1)
    ig = ml_g[..., :ML_HEADS] + b_igate
    lf = jax.nn.log_sigmoid((ml_g[..., ML_HEADS:] + b_fgate).astype(jnp.float32))
    hm = mlstm_chunkwise(ml_q.reshape(B, S, ML_HEADS, ML_D), ml_k.reshape(B, S, ML_HEADS, ML_D) * (ML_D ** -0.5),
                         ml_v.reshape(B, S, ML_HEADS, ML_D), ig, lf)
    hm = (jax.nn.sigmoid(ml_o) * rms_norm(hm, g_mlstm_head).reshape(B, S, ML_HEADS * ML_D)).astype(x.dtype)

    mem_kv = rms_norm(mem, g_mem) @ w_mem_kv
    mk, mv = jnp.split(mem_kv, 2, axis=-1)
    M = mem.shape[1]
    xa = memory_cross_attention(xa_q.reshape(B, S, XA_HEADS, XA_D), mk.reshape(B, M, XA_HEADS, XA_D),
                                mv.reshape(B, M, XA_HEADS, XA_D))
    xa = xa.reshape(B, S, XA_HEADS * XA_D).astype(x.dtype)

    gates = jax.nn.sigmoid(gate_pre + b_gate).reshape(B, S, N_BRANCH, D_MODEL)
    merged = (gates[..., 0, :] * (da @ w_branch_diff) + gates[..., 1, :] * (hm @ w_branch_mlstm)
              + gates[..., 2, :] * (xa @ w_branch_cross))
    x = x + merged @ w_out

    x = x + hierarchical_moe(rms_norm(x, g_ffn), w_router_group, b_router_group, w_router_expert,
                             b_router_expert, w_expert_gate, w_expert_up, w_expert_down)
    return x


def setup_inputs(seed: int = 0) -> dict:
    key = jax.random.key(seed)
    ks = jax.random.split(key, 32)
    L, D = DEPTH, D_MODEL
    f32 = jnp.float32

    def nrm(k, shape, scale):
        return jax.random.normal(k, shape, f32) * scale

    def gain(k, shape):
        return 1.0 + 0.02 * jax.random.normal(k, shape, f32)

    f_bias_base = jnp.linspace(3.0, 6.0, ML_HEADS, dtype=f32)
    return {
        "x": nrm(ks[0], (BATCH, SEQ, D), 1.0),
        "mem": nrm(ks[1], (BATCH, MEM_LEN, D), 1.0),
        "g_mix": gain(ks[2], (L, D)),
        "w_in": nrm(ks[3], (L, D, D_IN), D ** -0.5),
        "conv_w": nrm(ks[4], (L, CONV_W, 2 * ML_HEADS * ML_D), CONV_W ** -0.5),
        "conv_b": nrm(ks[5], (L, 2 * ML_HEADS * ML_D), 0.02),
        "b_igate": nrm(ks[6], (L, ML_HEADS), 0.1),
        "b_fgate": f_bias_base + nrm(ks[7], (L, ML_HEADS), 0.1),
        "lam_q1": nrm(ks[8], (L, DA_DK), 0.1),
        "lam_k1": nrm(ks[9], (L, DA_DK), 0.1),
        "lam_q2": nrm(ks[10], (L, DA_DK), 0.1),
        "lam_k2": nrm(ks[11], (L, DA_DK), 0.1),
        "g_diff_head": gain(ks[12], (L, DA_HEADS, DA_DV)),
        "g_mlstm_head": gain(ks[13], (L, ML_HEADS, ML_D)),
        "g_mem": gain(ks[14], (L, D)),
        "w_mem_kv": nrm(ks[15], (L, D, 2 * XA_HEADS * XA_D), D ** -0.5),
        "w_branch_diff": nrm(ks[16], (L, DA_HEADS * DA_DV, D), (DA_HEADS * DA_DV) ** -0.5),
        "w_branch_mlstm": nrm(ks[17], (L, ML_HEADS * ML_D, D), (ML_HEADS * ML_D) ** -0.5),
        "w_branch_cross": nrm(ks[18], (L, XA_HEADS * XA_D, D), (XA_HEADS * XA_D) ** -0.5),
        "b_gate": nrm(ks[19], (L, N_BRANCH * D), 0.02),
        "w_out": nrm(ks[20], (L, D, D), D ** -0.5),
        "g_ffn": gain(ks[21], (L, D)),
        "w_router_group": nrm(ks[22], (L, D, N_GROUPS), D ** -0.5),
        "b_router_group": nrm(ks[23], (L, N_GROUPS), 0.01),
        "w_router_expert": nrm(ks[24], (L, D, N_EXPERTS), D ** -0.5),
        "b_router_expert": nrm(ks[25], (L, N_EXPERTS), 0.01),
        "w_expert_gate": nrm(ks[26], (L, N_EXPERTS, D, D_EXPERT), D ** -0.5),
        "w_expert_up": nrm(ks[27], (L, N_EXPERTS, D, D_EXPERT), D ** -0.5),
        "w_expert_down": nrm(ks[28], (L, N_EXPERTS, D_EXPERT, D), D_EXPERT ** -0.5),
        "g_final": gain(ks[29], (D,)),
    }


def reference(x, mem, g_mix, w_in, conv_w, conv_b, b_igate, b_fgate, lam_q1, lam_k1, lam_q2, lam_k2,
              g_diff_head, g_mlstm_head, g_mem, w_mem_kv, w_branch_diff, w_branch_mlstm, w_branch_cross,
              b_gate, w_out, g_ffn, w_router_group, b_router_group, w_router_expert, b_router_expert,
              w_expert_gate, w_expert_up, w_expert_down, g_final):
    for l in range(DEPTH):
        x = hybrid_layer(x, mem, l, g_mix[l], w_in[l], conv_w[l], conv_b[l], b_igate[l], b_fgate[l],
                         lam_q1[l], lam_k1[l], lam_q2[l], lam_k2[l], g_diff_head[l], g_mlstm_head[l],
                         g_mem[l], w_mem_kv[l], w_branch_diff[l], w_branch_mlstm[l], w_branch_cross[l],
                         b_gate[l], w_out[l], g_ffn[l], w_router_group[l], b_router_group[l],
                         w_router_expert[l], b_router_expert[l], w_expert_gate[l], w_expert_up[l],
                         w_expert_down[l])
    return rms_norm(x, g_final)
```

```python
import functools
import math

import jax
import jax.numpy as jnp
from jax import lax
from jax.experimental import pallas as pl
from jax.experimental.pallas import tpu as pltpu

F32 = jnp.float32
BF16 = jnp.bfloat16

D_MODEL = 2048
EPS = 1e-6
DA_HEADS = 8
DA_DK = 128
DA_DV = 256
ML_HEADS = 4
ML_D = 256
CONV_W = 4
XA_HEADS = 4
XA_D = 256
N_BRANCH = 3
N_GROUPS = 4
EXPERTS_PER_GROUP = 8
N_EXPERTS = N_GROUPS * EXPERTS_PER_GROUP
TOP_K = 2
D_EXPERT = 704

LANES = 128
MXU_DIM = 256
D_EXPERT_PAD = 768
VMEM_LIMIT = 56 * 1024 * 1024

COL_DA_Q = 0
COL_DA_K = 2048
COL_DA_V = 4096
COL_ML_QK = 6144
COL_GATE = 8192
COL_ML_V = 14336
COL_ML_O = 15360
COL_XA_Q = 16384
D_PROJ = 17408

NEG_BIG = -1e30


def _cparams(sem):
    return pltpu.CompilerParams(dimension_semantics=sem, vmem_limit_bytes=VMEM_LIMIT)


def _sigmoid(x):
    return 1.0 / (1.0 + jnp.exp(-x))


def _silu(x):
    return x * _sigmoid(x)


def _log_sigmoid(x):
    return jnp.minimum(x, 0.0) - jnp.log(1.0 + jnp.exp(-jnp.abs(x)))


def _inproj_kernel(x_ref, g_ref, w_ref, wg_ref, o_ref, og_ref, h_scr, *, n_q_tiles):
    j = pl.program_id(1)

    @pl.when(j == 0)
    def _():
        x = x_ref[...]
        ms = jnp.mean(x * x, axis=-1, keepdims=True)
        h = x * lax.rsqrt(ms + EPS) * g_ref[...]
        h_scr[...] = h.astype(BF16)
        og_ref[...] = jnp.dot(h, wg_ref[...], precision=lax.Precision.HIGHEST,
                              preferred_element_type=F32)

    acc = jnp.dot(h_scr[...], w_ref[...], preferred_element_type=F32)
    scale = jnp.where(j < n_q_tiles, DA_DK ** -0.5, 1.0).astype(F32)
    o_ref[...] = (acc * scale).astype(BF16)


def _inproj(x2d, g_mix, w_main, w_gates):
    T = x2d.shape[0]
    tm = min(512, T)
    tn = 1024
    kern = functools.partial(_inproj_kernel, n_q_tiles=(COL_DA_K - COL_DA_Q) // tn)
    return pl.pallas_call(
        kern,
        grid=(T // tm, D_PROJ // tn),
        in_specs=[
            pl.BlockSpec((tm, D_MODEL), lambda i, j: (i, 0)),
            pl.BlockSpec((1, D_MODEL), lambda i, j: (0, 0)),
            pl.BlockSpec((D_MODEL, tn), lambda i, j: (0, j)),
            pl.BlockSpec((D_MODEL, LANES), lambda i, j: (0, 0)),
        ],
        out_specs=[
            pl.BlockSpec((tm, tn), lambda i, j: (i, j)),
            pl.BlockSpec((tm, LANES), lambda i, j: (i, 0)),
        ],
        out_shape=[
            jax.ShapeDtypeStruct((T, D_PROJ), BF16),
            jax.ShapeDtypeStruct((T, LANES), F32),
        ],
        scratch_shapes=[pltpu.VMEM((tm, D_MODEL), BF16)],
        compiler_params=_cparams(("parallel", "arbitrary")),
        name="inproj",
    )(x2d, g_mix.reshape(1, D_MODEL), w_main, w_gates)


def _diffattn_kernel(lam_ref, q_ref, k_ref, v_ref, g_ref, o_ref, m_scr, l_scr, acc_scr,
                     *, tq, lam_init):
    h = pl.program_id(1)
    qi = pl.program_id(2)
    lq = lam_ref[...]
    lam = (jnp.exp(jnp.sum(lq[0:1] * lq[1:2], axis=-1, keepdims=True))
           - jnp.exp(jnp.sum(lq[2:3] * lq[3:4], axis=-1, keepdims=True)) + lam_init)
    lane = lax.broadcasted_iota(jnp.int32, (1, DA_DK), 1)
    slope = jnp.sum(jnp.where(lane == h, lq[4:5], 0.0), axis=-1, keepdims=True)

    col = lax.broadcasted_iota(jnp.int32, (1, tq), 1)
    q_all = q_ref[...]

    def scores(m, j):
        qm = q_all[:, m * DA_DK:(m + 1) * DA_DK]
        km = k_ref[pl.ds(pl.multiple_of(j * tq, tq), tq), m * DA_DK:(m + 1) * DA_DK]
        s = lax.dot_general(qm, km, (((1,), (1,)), ((), ())), preferred_element_type=F32)
        bias = slope * (col + (j - qi) * tq).astype(F32)
        return s + bias

    def update(m, j, s, first):
        vj = v_ref[pl.ds(pl.multiple_of(j * tq, tq), tq), :]
        row_max = jnp.max(s, axis=-1, keepdims=True)
        if first:
            m_new = row_max
            p = jnp.exp(s - m_new)
            l_scr[m] = jnp.sum(p, axis=-1, keepdims=True)
            acc_scr[m] = jnp.dot(p.astype(BF16), vj, preferred_element_type=F32)
        else:
            m_old = m_scr[m]
            m_new = jnp.maximum(m_old, row_max)
            alpha = jnp.exp(m_old - m_new)
            p = jnp.exp(s - m_new)
            l_scr[m] = alpha * l_scr[m] + jnp.sum(p, axis=-1, keepdims=True)
            acc_scr[m] = alpha * acc_scr[m] + jnp.dot(p.astype(BF16), vj,
                                                      preferred_element_type=F32)
        m_scr[m] = m_new

    row_i = lax.broadcasted_iota(jnp.int32, (tq, tq), 0)
    col_i = lax.broadcasted_iota(jnp.int32, (tq, tq), 1)
    causal = col_i <= row_i
    for m in range(2):
        s = jnp.where(causal, scores(m, qi), NEG_BIG)
        update(m, qi, s, True)

    def body(j, carry):
        for m in range(2):
            update(m, j, scores(m, j), False)
        return carry

    lax.fori_loop(0, qi, body, 0)

    o = acc_scr[0] / l_scr[0] - lam * (acc_scr[1] / l_scr[1])
    ms = jnp.mean(o * o, axis=-1, keepdims=True)
    o = o * lax.rsqrt(ms + EPS) * g_ref[0] * (1.0 - lam_init)
    o_ref[...] = o.astype(BF16)


def _diffattn(proj, lam_vecs, g_diff_head, B, S, lam_init):
    T = B * S
    tq = min(512, S)
    nq = S // tq
    kern = functools.partial(_diffattn_kernel, tq=tq, lam_init=lam_init)
    kblk = COL_DA_K // (2 * DA_DK)
    vblk = COL_DA_V // DA_DV
    return pl.pallas_call(
        kern,
        grid=(B, DA_HEADS, nq),
        in_specs=[
            pl.BlockSpec((8, DA_DK), lambda b, h, i: (0, 0)),
            pl.BlockSpec((tq, 2 * DA_DK), lambda b, h, i: (b * nq + i, h)),
            pl.BlockSpec((S, 2 * DA_DK), lambda b, h, i: (b, kblk + h)),
            pl.BlockSpec((S, DA_DV), lambda b, h, i: (b, vblk + h)),
            pl.BlockSpec((1, 1, DA_DV), lambda b, h, i: (h, 0, 0)),
        ],
        out_specs=pl.BlockSpec((tq, DA_DV), lambda b, h, i: (b * nq + i, h)),
        out_shape=jax.ShapeDtypeStruct((T, DA_HEADS * DA_DV), BF16),
        scratch_shapes=[
            pltpu.VMEM((2, tq, 1), F32),
            pltpu.VMEM((2, tq, 1), F32),
            pltpu.VMEM((2, tq, DA_DV), F32),
        ],
        compiler_params=_cparams(("parallel", "parallel", "arbitrary")),
        name="diffattn",
    )(lam_vecs, proj, proj, proj, g_diff_head.reshape(DA_HEADS, 1, DA_DV))


def _mlstm_kernel(qk_ref, v_ref, og_ref, gt_ref, cw_ref, cb_ref, big_ref, bfg_ref, bgcol_ref,
                  gh_ref, o_ref, prev_scr, c_scr, n_scr, m_scr, *, L):
    c = pl.program_id(1)

    @pl.when(c == 0)
    def _():
        prev_scr[...] = jnp.zeros_like(prev_scr)
        c_scr[...] = jnp.zeros_like(c_scr)
        n_scr[...] = jnp.zeros_like(n_scr)
        m_scr[...] = jnp.zeros_like(m_scr)

    cur = qk_ref[...].astype(F32)
    prev = prev_scr[...]
    cw = cw_ref[...]
    row = lax.broadcasted_iota(jnp.int32, (L, 1), 0)
    conv = cur * cw[CONV_W - 1:CONV_W] + cb_ref[...]
    for s in range(1, CONV_W):
        shifted = jnp.where(row < s, pltpu.roll(prev, s, 0), pltpu.roll(cur, s, 0))
        conv = conv + shifted * cw[CONV_W - 1 - s:CONV_W - s]
    prev_scr[...] = cur
    qk = _silu(conv)

    gcol = gt_ref[...]
    grow = jnp.transpose(gcol)[0:8, :] + bgcol_ref[...]
    lane = lax.broadcasted_iota(jnp.int32, (L, LANES), 1)
    gcol = gcol + jnp.where(lane < ML_HEADS, big_ref[...], bfg_ref[...])
    r_i = lax.broadcasted_iota(jnp.int32, (L, L), 0)
    c_i = lax.broadcasted_iota(jnp.int32, (L, L), 1)
    tril = (c_i <= r_i).astype(F32)
    triu = (r_i <= c_i).astype(F32)
    lf_col = _log_sigmoid(gcol)
    lf_row = _log_sigmoid(grow)
    b_col_all = jnp.dot(tril, lf_col, precision=lax.Precision.HIGHEST,
                        preferred_element_type=F32)
    b_row_all = jnp.dot(lf_row, triu, precision=lax.Precision.HIGHEST,
                        preferred_element_type=F32)
    causal = c_i <= r_i

    vv = v_ref[...]
    og = og_ref[...].astype(F32)
    for hd in range(ML_HEADS):
        q = qk[:, hd * ML_D:(hd + 1) * ML_D]
        k = qk[:, (ML_HEADS + hd) * ML_D:(ML_HEADS + hd + 1) * ML_D] * (ML_D ** -0.5)
        v = vv[:, hd * ML_D:(hd + 1) * ML_D]
        qb = q.astype(BF16)
        kb = k.astype(BF16)
        ig_col = gcol[:, hd:hd + 1]
        ig_row = grow[hd:hd + 1, :]
        b_col = b_col_all[:, ML_HEADS + hd:ML_HEADS + hd + 1]
        b_row = b_row_all[ML_HEADS + hd:ML_HEADS + hd + 1, :]
        m_prev = m_scr[hd]
        C = c_scr[hd]
        n = n_scr[hd]

        log_d = jnp.where(causal, b_col - b_row + ig_row, -jnp.inf)
        m_inter = b_col + m_prev
        m_t = jnp.maximum(m_inter, jnp.max(log_d, axis=-1, keepdims=True))
        dmat = jnp.exp(log_d - m_t)
        inter = jnp.exp(m_inter - m_t)
        w = lax.dot_general(qb, kb, (((1,), (1,)), ((), ())), preferred_element_type=F32) * dmat
        num = (jnp.dot(w.astype(BF16), v, preferred_element_type=F32)
               + inter * jnp.dot(qb, C.astype(BF16), preferred_element_type=F32))
        nq = jnp.sum(w, axis=-1, keepdims=True) + inter * jnp.sum(q * n, axis=-1, keepdims=True)
        hh = num / jnp.maximum(jnp.abs(nq), jnp.exp(-m_t))

        b_last = b_row[:, L - 1:L]
        a_row = b_last - b_row + ig_row
        a_col = b_last - b_col + ig_col
        m_new = jnp.maximum(b_last + m_prev, jnp.max(a_row, axis=-1, keepdims=True))
        decay = jnp.exp(b_last + m_prev - m_new)
        kw = jnp.exp(a_col - m_new) * k
        c_scr[hd] = decay * C + jnp.dot(jnp.transpose(kw).astype(BF16), v,
                                        preferred_element_type=F32)
        n_scr[hd] = decay * n + jnp.sum(kw, axis=0, keepdims=True)
        m_scr[hd] = m_new

        ms = jnp.mean(hh * hh, axis=-1, keepdims=True)
        hn = hh * lax.rsqrt(ms + EPS) * gh_ref[hd]
        o_ref[:, hd * ML_D:(hd + 1) * ML_D] = (
            _sigmoid(og[:, hd * ML_D:(hd + 1) * ML_D]) * hn).astype(BF16)


def _mlstm(proj, gates, conv_w, conv_b, b_igate, b_fgate, g_mlstm_head, B, S):
    T = B * S
    L = min(256, S)
    nc = S // L
    dqk = 2 * ML_HEADS * ML_D
    dv = ML_HEADS * ML_D
    big = jnp.zeros((1, LANES), F32).at[0, :ML_HEADS].set(b_igate)
    bfg = jnp.zeros((1, LANES), F32).at[0, ML_HEADS:2 * ML_HEADS].set(b_fgate)
    bgcol = jnp.concatenate([b_igate, b_fgate]).reshape(2 * ML_HEADS, 1)
    kern = functools.partial(_mlstm_kernel, L=L)
    return pl.pallas_call(
        kern,
        grid=(B, nc),
        in_specs=[
            pl.BlockSpec((L, dqk), lambda b, c: (b * nc + c, COL_ML_QK // dqk)),
            pl.BlockSpec((L, dv), lambda b, c: (b * nc + c, COL_ML_V // dv)),
            pl.BlockSpec((L, dv), lambda b, c: (b * nc + c, COL_ML_O // dv)),
            pl.BlockSpec((L, LANES), lambda b, c: (b * nc + c, 0)),
            pl.BlockSpec((CONV_W, dqk), lambda b, c: (0, 0)),
            pl.BlockSpec((1, dqk), lambda b, c: (0, 0)),
            pl.BlockSpec((1, LANES), lambda b, c: (0, 0)),
            pl.BlockSpec((1, LANES), lambda b, c: (0, 0)),
            pl.BlockSpec((2 * ML_HEADS, 1), lambda b, c: (0, 0)),
            pl.BlockSpec((ML_HEADS, 1, ML_D), lambda b, c: (0, 0, 0)),
        ],
        out_specs=pl.BlockSpec((L, dv), lambda b, c: (b * nc + c, 0)),
        out_shape=jax.ShapeDtypeStruct((T, dv), BF16),
        scratch_shapes=[
            pltpu.VMEM((L, dqk), F32),
            pltpu.VMEM((ML_HEADS, ML_D, ML_D), F32),
            pltpu.VMEM((ML_HEADS, 1, ML_D), F32),
            pltpu.VMEM((ML_HEADS, 1, 1), F32),
        ],
        compiler_params=_cparams(("parallel", "arbitrary")),
        name="mlstm",
    )(proj, proj, proj, gates, conv_w, conv_b.reshape(1, dqk), big, bfg, bgcol,
      g_mlstm_head.reshape(ML_HEADS, 1, ML_D))


def _memkv_kernel(x_ref, g_ref, w_ref, o_ref):
    x = x_ref[...]
    ms = jnp.mean(x * x, axis=-1, keepdims=True)
    h = (x * lax.rsqrt(ms + EPS) * g_ref[...]).astype(BF16)
    o_ref[...] = jnp.dot(h, w_ref[...], preferred_element_type=F32).astype(BF16)


def _memkv(mem2d, g_mem, w_kv):
    R = mem2d.shape[0]
    N = w_kv.shape[1]
    tn = 512
    return pl.pallas_call(
        _memkv_kernel,
        grid=(N // tn,),
        in_specs=[
            pl.BlockSpec((R, D_MODEL), lambda j: (0, 0)),
            pl.BlockSpec((1, D_MODEL), lambda j: (0, 0)),
            pl.BlockSpec((D_MODEL, tn), lambda j: (0, j)),
        ],
        out_specs=pl.BlockSpec((R, tn), lambda j: (0, j)),
        out_shape=jax.ShapeDtypeStruct((R, N), BF16),
        compiler_params=_cparams(("parallel",)),
        name="memkv",
    )(mem2d, g_mem.reshape(1, D_MODEL), w_kv)


def _merge_kernel(da_ref, hm_ref, xq_ref, g0_ref, g1_ref, g2_ref, mkv_ref, wd_ref, wm_ref, wc_ref,
                  bg_ref, o_ref):
    dxa = XA_HEADS * XA_D
    xq = xq_ref[...]
    parts = []
    for hd in range(XA_HEADS):
        q = xq[:, hd * XA_D:(hd + 1) * XA_D]
        mk = mkv_ref[:, hd * XA_D:(hd + 1) * XA_D]
        mv = mkv_ref[:, dxa + hd * XA_D:dxa + (hd + 1) * XA_D]
        s = lax.dot_general(q, mk, (((1,), (1,)), ((), ())),
                            preferred_element_type=F32) * (XA_D ** -0.5)
        s = s - jnp.max(s, axis=-1, keepdims=True)
        p = jnp.exp(s)
        p = p / jnp.sum(p, axis=-1, keepdims=True)
        parts.append(jnp.dot(p.astype(BF16), mv, preferred_element_type=F32).astype(BF16))
    xa = jnp.concatenate(parts, axis=-1)

    bg = bg_ref[...]
    acc = _sigmoid(g0_ref[...].astype(F32) + bg[0:1]) * jnp.dot(
        da_ref[...], wd_ref[...], preferred_element_type=F32)
    acc = acc + _sigmoid(g1_ref[...].astype(F32) + bg[1:2]) * jnp.dot(
        hm_ref[...], wm_ref[...], preferred_element_type=F32)
    acc = acc + _sigmoid(g2_ref[...].astype(F32) + bg[2:3]) * jnp.dot(
        xa, wc_ref[...], preferred_element_type=F32)
    o_ref[...] = acc.astype(BF16)


def _merge(proj, da, hm, mkv, w_d, w_m, w_c, b_gate, B, S, M):
    T = B * S
    tm = min(256, S)
    nt = S // tm
    dxa = XA_HEADS * XA_D
    dml = ML_HEADS * ML_D
    gblk = COL_GATE // D_MODEL
    const = lambda i: (0, 0)
    return pl.pallas_call(
        _merge_kernel,
        grid=(T // tm,),
        in_specs=[
            pl.BlockSpec((tm, DA_HEADS * DA_DV), lambda i: (i, 0)),
            pl.BlockSpec((tm, dml), lambda i: (i, 0)),
            pl.BlockSpec((tm, dxa), lambda i: (i, COL_XA_Q // dxa)),
            pl.BlockSpec((tm, D_MODEL), lambda i: (i, gblk)),
            pl.BlockSpec((tm, D_MODEL), lambda i: (i, gblk + 1)),
            pl.BlockSpec((tm, D_MODEL), lambda i: (i, gblk + 2)),
            pl.BlockSpec((M, 2 * dxa), lambda i: (i // nt, 0)),
            pl.BlockSpec((DA_HEADS * DA_DV, D_MODEL), const),
            pl.BlockSpec((dml, D_MODEL), const),
            pl.BlockSpec((dxa, D_MODEL), const),
            pl.BlockSpec((N_BRANCH, D_MODEL), const),
        ],
        out_specs=pl.BlockSpec((tm, D_MODEL), lambda i: (i, 0)),
        out_shape=jax.ShapeDtypeStruct((T, D_MODEL), BF16),
        compiler_params=_cparams(("parallel",)),
        name="merge",
    )(da, hm, proj, proj, proj, proj, mkv, w_d, w_m, w_c, b_gate.reshape(N_BRANCH, D_MODEL))


def _outproj_kernel(x_ref, mg_ref, wo_ref, gf_ref, wr_ref, br_ref, x1_ref, h2_ref, rt_ref):
    x1 = x_ref[...] + jnp.dot(mg_ref[...], wo_ref[...], preferred_element_type=F32)
    x1_ref[...] = x1
    ms = jnp.mean(x1 * x1, axis=-1, keepdims=True)
    h2 = x1 * lax.rsqrt(ms + EPS) * gf_ref[...]
    h2_ref[...] = h2
    logits = jnp.dot(h2, wr_ref[...], precision=lax.Precision.HIGHEST,
                     preferred_element_type=F32) + br_ref[...]
    lane = lax.broadcasted_iota(jnp.int32, logits.shape, 1)
    lane_f = lane.astype(F32)
    big = float(LANES)
    gl = jnp.where(lane < N_GROUPS, logits, -jnp.inf)
    gmax = jnp.max(gl, axis=-1, keepdims=True)
    grp = jnp.min(jnp.where(gl == gmax, lane_f, big), axis=-1, keepdims=True)
    p_grp = 1.0 / jnp.sum(jnp.exp(gl - gmax), axis=-1, keepdims=True)
    e_idx = lane - N_GROUPS
    in_grp = (e_idx >= 0) & (e_idx < N_EXPERTS) & (
        lax.shift_right_arithmetic(e_idx, 3).astype(F32) == grp)
    el = jnp.where(in_grp, logits, -jnp.inf)
    v1 = jnp.max(el, axis=-1, keepdims=True)
    i1 = jnp.min(jnp.where(el == v1, lane_f, big), axis=-1, keepdims=True)
    el2 = jnp.where(lane_f == i1, -jnp.inf, el)
    v2 = jnp.max(el2, axis=-1, keepdims=True)
    i2 = jnp.min(jnp.where(el2 == v2, lane_f, big), axis=-1, keepdims=True)
    t = jnp.exp(v2 - v1)
    w1 = p_grp / (1.0 + t)
    w2 = p_grp * t / (1.0 + t)
    rt = jnp.where(lane == 0, i1 - N_GROUPS,
                   jnp.where(lane == 1, i2 - N_GROUPS,
                             jnp.where(lane == 2, w1, jnp.where(lane == 3, w2, 0.0))))
    rt_ref[...] = rt


def _outproj(x2d, merged, w_out, g_ffn, w_route, b_route):
    T = x2d.shape[0]
    tm = min(256, T)
    const = lambda i: (0, 0)
    return pl.pallas_call(
        _outproj_kernel,
        grid=(T // tm,),
        in_specs=[
            pl.BlockSpec((tm, D_MODEL), lambda i: (i, 0)),
            pl.BlockSpec((tm, D_MODEL), lambda i: (i, 0)),
            pl.BlockSpec((D_MODEL, D_MODEL), const),
            pl.BlockSpec((1, D_MODEL), const),
            pl.BlockSpec((D_MODEL, LANES), const),
            pl.BlockSpec((1, LANES), const),
        ],
        out_specs=[
            pl.BlockSpec((tm, D_MODEL), lambda i: (i, 0)),
            pl.BlockSpec((tm, D_MODEL), lambda i: (i, 0)),
            pl.BlockSpec((tm, LANES), lambda i: (i, 0)),
        ],
        out_shape=[
            jax.ShapeDtypeStruct((T, D_MODEL), F32),
            jax.ShapeDtypeStruct((T, D_MODEL), F32),
            jax.ShapeDtypeStruct((T, LANES), F32),
        ],
        compiler_params=_cparams(("parallel",)),
        name="outproj",
    )(x2d, merged, w_out, g_ffn.reshape(1, D_MODEL), w_route, b_route)


def _experts_kernel(blk_e_ref, tok_ref, nused_ref, h2_hbm, wg_ref, wu_ref, wd_ref, bw_ref,
                    o_ref, xbuf, sem, *, blk):
    i = pl.program_id(0)
    nused = nused_ref[0]

    def row_copy(tok, slot, r):
        return pltpu.make_async_copy(h2_hbm.at[pl.ds(tok, 1), :],
                                     xbuf.at[slot, pl.ds(r, 1), :], sem.at[slot])

    def issue(b, slot):
        def body(r, carry):
            row_copy(tok_ref[b * blk + r], slot, r).start()
            return carry
        lax.fori_loop(0, blk, body, 0)

    def wait(slot):
        def body(r, carry):
            row_copy(0, slot, r).wait()
            return carry
        lax.fori_loop(0, blk, body, 0)

    @pl.when(i == 0)
    def _():
        issue(0, 0)

    @pl.when(i + 1 < nused)
    def _():
        issue(i + 1, (i + 1) % 2)

    @pl.when(i < nused)
    def _():
        slot = i % 2
        wait(slot)
        x = xbuf[slot].astype(BF16)
        a = jnp.dot(x, wg_ref[0], preferred_element_type=F32)
        u = jnp.dot(x, wu_ref[0], preferred_element_type=F32)
        hmid = (_silu(a) * u).astype(BF16)
        y = jnp.dot(hmid, wd_ref[0], preferred_element_type=F32)
        o_ref[...] = y * bw_ref[...]

    @pl.when(i >= nused)
    def _():
        o_ref[...] = jnp.zeros_like(o_ref)


def _experts(h2, blk_e, buf_tok, nused, buf_w, w_g, w_u, w_d, blk):
    P = buf_tok.shape[0]
    nb = P // blk
    kern = functools.partial(_experts_kernel, blk=blk)
    grid_spec = pltpu.PrefetchScalarGridSpec(
        num_scalar_prefetch=3,
        grid=(nb,),
        in_specs=[
            pl.BlockSpec(memory_space=pl.ANY),
            pl.BlockSpec((1, D_MODEL, D_EXPERT_PAD), lambda i, be, tk, nu: (be[i], 0, 0)),
            pl.BlockSpec((1, D_MODEL, D_EXPERT_PAD), lambda i, be, tk, nu: (be[i], 0, 0)),
            pl.BlockSpec((1, D_EXPERT_PAD, D_MODEL), lambda i, be, tk, nu: (be[i], 0, 0)),
            pl.BlockSpec((blk, 1), lambda i, be, tk, nu: (i, 0)),
        ],
        out_specs=pl.BlockSpec((blk, D_MODEL), lambda i, be, tk, nu: (i, 0)),
        scratch_shapes=[
            pltpu.VMEM((2, blk, D_MODEL), F32),
            pltpu.SemaphoreType.DMA((2,)),
        ],
    )
    return pl.pallas_call(
        kern,
        grid_spec=grid_spec,
        out_shape=jax.ShapeDtypeStruct((P, D_MODEL), F32),
        compiler_params=_cparams(("arbitrary",)),
        name="experts",
    )(blk_e, buf_tok, nused, h2, w_g, w_u, w_d, buf_w.reshape(P, 1))


def _combine_kernel(pos_ref, x1_ref, eo_hbm, g_ref, o_ref, ybuf, sem, *, tm, nsteps):
    i = pl.program_id(0)

    def row_copy(p, slot, r):
        return pltpu.make_async_copy(eo_hbm.at[pl.ds(p, 1), :],
                                     ybuf.at[slot, pl.ds(r, 1), :], sem.at[slot])

    def issue(step, slot):
        def body(r, carry):
            row_copy(pos_ref[step * (TOP_K * tm) + r], slot, r).start()
            return carry
        lax.fori_loop(0, TOP_K * tm, body, 0)

    def wait(slot):
        def body(r, carry):
            row_copy(0, slot, r).wait()
            return carry
        lax.fori_loop(0, TOP_K * tm, body, 0)

    @pl.when(i == 0)
    def _():
        issue(0, 0)

    @pl.when(i + 1 < nsteps)
    def _():
        issue(i + 1, (i + 1) % 2)

    slot = i % 2
    wait(slot)
    y = ybuf[slot]
    x2 = x1_ref[...] + y[0:tm] + y[tm:2 * tm]
    ms = jnp.mean(x2 * x2, axis=-1, keepdims=True)
    o_ref[...] = x2 * lax.rsqrt(ms + EPS) * g_ref[...]


def _combine(x1, eo, pos_steps, g_final, tm):
    T = x1.shape[0]
    nsteps = T // tm
    kern = functools.partial(_combine_kernel, tm=tm, nsteps=nsteps)
    grid_spec = pltpu.PrefetchScalarGridSpec(
        num_scalar_prefetch=1,
        grid=(nsteps,),
        in_specs=[
            pl.BlockSpec((tm, D_MODEL), lambda i, pos: (i, 0)),
            pl.BlockSpec(memory_space=pl.ANY),
            pl.BlockSpec((1, D_MODEL), lambda i, pos: (0, 0)),
        ],
        out_specs=pl.BlockSpec((tm, D_MODEL), lambda i, pos: (i, 0)),
        scratch_shapes=[
            pltpu.VMEM((2, TOP_K * tm, D_MODEL), F32),
            pltpu.SemaphoreType.DMA((2,)),
        ],
    )
    return pl.pallas_call(
        kern,
        grid_spec=grid_spec,
        out_shape=jax.ShapeDtypeStruct((T, D_MODEL), F32),
        compiler_params=_cparams(("arbitrary",)),
        name="combine",
    )(pos_steps, x1, eo, g_final.reshape(1, D_MODEL))


def _dispatch_tables(route, T, blk):
    eid = route[:, 0:TOP_K].astype(jnp.int32).reshape(-1)
    wts = route[:, TOP_K:2 * TOP_K].reshape(-1)
    n_assign = T * TOP_K
    onehot = (eid[:, None] == jnp.arange(N_EXPERTS, dtype=jnp.int32)[None, :]).astype(jnp.int32)
    csum = jnp.cumsum(onehot, axis=0)
    rank = jnp.take_along_axis(csum, eid[:, None], axis=1)[:, 0] - 1
    counts = csum[-1]
    padded = (counts + blk - 1) // blk * blk
    pad_end = jnp.cumsum(padded)
    pad_start = pad_end - padded
    dest = pad_start[eid] + rank
    nb = n_assign // blk + N_EXPERTS
    P = nb * blk
    tok = jnp.arange(n_assign, dtype=jnp.int32) // TOP_K
    buf_tok = jnp.zeros((P,), jnp.int32).at[dest].set(tok)
    buf_w = jnp.zeros((P,), F32).at[dest].set(wts)
    blk_e = jnp.minimum(
        jnp.searchsorted(pad_end, jnp.arange(nb, dtype=jnp.int32) * blk, side='right'),
        N_EXPERTS - 1).astype(jnp.int32)
    nused = (pad_end[-1] // blk).astype(jnp.int32).reshape(1)
    return blk_e, buf_tok, buf_w, nused, dest.reshape(T, TOP_K)


def _layer(x, mem, layer_idx, g_mix, w_in, conv_w, conv_b, b_igate, b_fgate, lam_q1, lam_k1,
           lam_q2, lam_k2, g_diff_head, g_mlstm_head, g_mem, w_mem_kv, w_branch_diff,
           w_branch_mlstm, w_branch_cross, b_gate, w_out, g_ffn, w_router_group, b_router_group,
           w_router_expert, b_router_expert, w_expert_gate, w_expert_up, w_expert_down, g_out):
    B, S, _ = x.shape
    M = mem.shape[1]
    T = B * S
    x2d = x.reshape(T, D_MODEL)

    c = w_in
    w_main = jnp.concatenate([
        c[:, 0:8192],
        c[:, 11272:17416],
        c[:, 8192:10240],
        c[:, 10248:11272],
    ], axis=1).astype(BF16)
    w_gates = jnp.pad(c[:, 10240:10248], ((0, 0), (0, LANES - 2 * ML_HEADS)))
    slopes = 2.0 ** (-8.0 * jnp.arange(1, DA_HEADS + 1, dtype=F32) / DA_HEADS)
    lam_vecs = jnp.zeros((8, DA_DK), F32).at[0:4].set(
        jnp.stack([lam_q1, lam_k1, lam_q2, lam_k2]).astype(F32)).at[4, :DA_HEADS].set(slopes)
    lam_init = 0.8 - 0.6 * math.exp(-0.3 * layer_idx)
    pad_e = D_EXPERT_PAD - D_EXPERT
    w_eg = jnp.pad(w_expert_gate, ((0, 0), (0, 0), (0, pad_e))).astype(BF16)
    w_eu = jnp.pad(w_expert_up, ((0, 0), (0, 0), (0, pad_e))).astype(BF16)
    w_ed = jnp.pad(w_expert_down, ((0, 0), (0, pad_e), (0, 0))).astype(BF16)
    w_route = jnp.pad(jnp.concatenate([w_router_group, w_router_expert], axis=1),
                      ((0, 0), (0, LANES - N_GROUPS - N_EXPERTS)))
    b_route = jnp.pad(jnp.concatenate([b_router_group, b_router_expert]),
                      (0, LANES - N_GROUPS - N_EXPERTS)).reshape(1, LANES)

    proj, gates = _inproj(x2d, g_mix, w_main, w_gates)
    da = _diffattn(proj, lam_vecs, g_diff_head, B, S, lam_init)
    hm = _mlstm(proj, gates, conv_w, conv_b, b_igate, b_fgate, g_mlstm_head, B, S)
    mkv = _memkv(mem.reshape(B * M, D_MODEL), g_mem, w_mem_kv.astype(BF16))
    merged = _merge(proj, da, hm, mkv, w_branch_diff.astype(BF16), w_branch_mlstm.astype(BF16),
                    w_branch_cross.astype(BF16), b_gate, B, S, M)
    x1, h2, route = _outproj(x2d, merged, w_out.astype(BF16), g_ffn, w_route, b_route)

    blk = min(256, T)
    blk_e, buf_tok, buf_w, nused, pos = _dispatch_tables(route, T, blk)
    eo = _experts(h2, blk_e, buf_tok, nused, buf_w, w_eg, w_eu, w_ed, blk)

    tm = min(256, T)
    pos_steps = pos.reshape(T // tm, tm, TOP_K).transpose(0, 2, 1).reshape(-1)
    out = _combine(x1, eo, pos_steps, g_out, tm)
    return out.reshape(B, S, D_MODEL)


def kernel(x, mem, g_mix, w_in, conv_w, conv_b, b_igate, b_fgate, lam_q1, lam_k1, lam_q2, lam_k2,
           g_diff_head, g_mlstm_head, g_mem, w_mem_kv, w_branch_diff, w_branch_mlstm,
           w_branch_cross, b_gate, w_out, g_ffn, w_router_group, b_router_group, w_router_expert,
           b_router_expert, w_expert_gate, w_expert_up, w_expert_down, g_final):
    depth = g_mix.shape[0]
    assert depth == 1, "the fused final norm assumes a single layer"
    l = 0
    return _layer(x, mem, l, g_mix[l], w_in[l], conv_w[l], conv_b[l], b_igate[l], b_fgate[l],
                  lam_q1[l], lam_k1[l], lam_q2[l], lam_k2[l], g_diff_head[l], g_mlstm_head[l],
                  g_mem[l], w_mem_kv[l], w_branch_diff[l], w_branch_mlstm[l], w_branch_cross[l],
                  b_gate[l], w_out[l], g_ffn[l], w_router_group[l], b_router_group[l],
                  w_router_expert[l], b_router_expert[l], w_expert_gate[l], w_expert_up[l],
                  w_expert_down[l], g_final)
```
